```python
import jax, jax.numpy as jnp
from jax import lax
import numpy as np

D_MODEL = 2048
BATCH = 8
SEQ = 2048
DEPTH = 1
DEC_BATCH = 32
DEC_SEQ = 32
PAST_LEN = 4096

CHUNK = 64
D_MIX = 2048
GM_WIDTH = 1024
GM_HEADS = 8
GM_HEAD_DIM = GM_WIDTH // GM_HEADS
GM_CHUNK = 128
SSM_WIDTH = 1024
SSM_GROUP = 16
SSM_GROUPS = SSM_WIDTH // SSM_GROUP
SSM_STATE = 64
N_EXPERTS = 32
TOP_K = 4
D_FF = 2048
SWIGLU_LIMIT = 7.0
SWIGLU_ALPHA = 1.702
MOE_BLOCK = 128
NORM_EPS = 1e-5
DT_MIN = 0.001
DT_MAX = 0.1

kernel_name = 'hymba_gmlp_s5_moe_stream_step'


def rmsnorm(x, g):
    xf = x.astype(jnp.float32)
    y = xf * lax.rsqrt(jnp.mean(xf * xf, axis=-1, keepdims=True) + NORM_EPS)
    return (y * g.astype(jnp.float32)).astype(x.dtype)


def group_layernorm(v, g, b):
    vf = v.astype(jnp.float32)
    mu = jnp.mean(vf, axis=-1, keepdims=True)
    var = jnp.mean(jnp.square(vf - mu), axis=-1, keepdims=True)
    y = (vf - mu) * lax.rsqrt(var + NORM_EPS)
    return (y * g.astype(jnp.float32) + b.astype(jnp.float32)).astype(v.dtype)


def chunk_mask():
    pos = jnp.arange(GM_CHUNK)
    return (pos[None, :] // CHUNK) <= (pos[:, None] // CHUNK)


def gmlp_spatial(u, v, ws, bs):
    B, T = v.shape[0], v.shape[1]
    n = -(-T // GM_CHUNK)
    pad = n * GM_CHUNK - T
    vp = jnp.pad(v, ((0, 0), (0, pad), (0, 0), (0, 0))).reshape(B, n, GM_CHUNK, GM_HEADS, GM_HEAD_DIM)
    wm = jnp.where(chunk_mask()[None], ws, jnp.zeros_like(ws)).astype(v.dtype)
    s = jnp.einsum('hij,bnjhd->bnihd', wm, vp) + bs.T.astype(v.dtype)[None, None, :, :, None]
    s = s.reshape(B, n * GM_CHUNK, GM_HEADS, GM_HEAD_DIM)[:, :T]
    return u * s


def gmlp_mixer(z_u, z_v, ln_g, ln_b, ws, bs):
    B, S = z_u.shape[0], z_u.shape[1]
    u = jax.nn.gelu(z_u).reshape(B, S, GM_HEADS, GM_HEAD_DIM)
    v = group_layernorm(jax.nn.gelu(z_v).reshape(B, S, GM_HEADS, GM_HEAD_DIM),
                        ln_g.reshape(GM_HEADS, GM_HEAD_DIM), ln_b.reshape(GM_HEADS, GM_HEAD_DIM))
    out = gmlp_spatial(u, v, ws, bs).reshape(B, S, GM_WIDTH)
    return out, v.reshape(B, S, GM_WIDTH)


def _complex_affine_combine(e1, e2):
    a1r, a1i, b1r, b1i = e1
    a2r, a2i, b2r, b2i = e2
    return (a2r * a1r - a2i * a1i,
            a2r * a1i + a2i * a1r,
            a2r * b1r - a2i * b1i + b2r,
            a2r * b1i + a2i * b1r + b2i)


def s5_mixer(u, h0_re, h0_im, lam_re, lam_im, log_dt, b_re, b_im, c_re, c_im, d_skip, glu_w, glu_b):
    B, S = u.shape[0], u.shape[1]
    f32 = jnp.float32
    uf = u.astype(f32).reshape(B, S, SSM_GROUPS, SSM_GROUP)
    lr = lam_re.astype(f32)
    li = lam_im.astype(f32)
    dt = jnp.exp(log_dt.astype(f32))[:, None]
    mag = jnp.exp(lr * dt)
    ab_re = mag * jnp.cos(li * dt)
    ab_im = mag * jnp.sin(li * dt)
    den = lr * lr + li * li
    f_re = ((ab_re - 1.0) * lr + ab_im * li) / den
    f_im = (ab_im * lr - (ab_re - 1.0) * li) / den
    br = b_re.astype(f32)
    bi = b_im.astype(f32)
    bb_re = f_re[..., None] * br - f_im[..., None] * bi
    bb_im = f_re[..., None] * bi + f_im[..., None] * br
    bu_re = jnp.einsum('bsgc,gpc->bsgp', uf, bb_re)
    bu_im = jnp.einsum('bsgc,gpc->bsgp', uf, bb_im)
    h0r = h0_re.astype(f32)
    h0i = h0_im.astype(f32)
    bu_re = bu_re.at[:, 0].add(ab_re * h0r - ab_im * h0i)
    bu_im = bu_im.at[:, 0].add(ab_re * h0i + ab_im * h0r)
    a_re = jnp.broadcast_to(ab_re, bu_re.shape)
    a_im = jnp.broadcast_to(ab_im, bu_im.shape)
    _, _, xr, xi = lax.associative_scan(_complex_affine_combine, (a_re, a_im, bu_re, bu_im), axis=1)
    y = (jnp.einsum('bsgp,gcp->bsgc', xr, c_re.astype(f32))
         - jnp.einsum('bsgp,gcp->bsgc', xi, c_im.astype(f32))
         + d_skip.astype(f32).reshape(SSM_GROUPS, SSM_GROUP) * uf)
    y = jax.nn.gelu(y.reshape(B, S, SSM_WIDTH))
    y = y * jax.nn.sigmoid(y @ glu_w.astype(f32) + glu_b.astype(f32))
    return y.astype(u.dtype), xr[:, -1], xi[:, -1]


def moe(h, router_w, router_b, w1, b1, w2, b2):
    B, S, D = h.shape
    N = B * S
    hf = h.reshape(N, D)
    logits = (hf @ router_w).astype(jnp.float32) + router_b.astype(jnp.float32)
    top_val, top_idx = lax.top_k(logits, TOP_K)
    gates = jax.nn.softmax(top_val, axis=-1)
    A = N * TOP_K
    flat_e = top_idx.reshape(A)
    flat_tok = jnp.arange(A, dtype=jnp.int32) // TOP_K
    order = jnp.argsort(flat_e)
    sorted_e = flat_e[order]
    counts = jnp.bincount(flat_e, length=N_EXPERTS)
    padded = (counts + MOE_BLOCK - 1) // MOE_BLOCK * MOE_BLOCK
    pad_end = jnp.cumsum(padded)
    pad_start = pad_end - padded
    grp_start = jnp.cumsum(counts) - counts
    rank = jnp.arange(A) - grp_start[sorted_e]
    dest = pad_start[sorted_e] + rank
    NB = -(-A // MOE_BLOCK) + N_EXPERTS
    P = NB * MOE_BLOCK
    row_tok = jnp.full((P,), N, dtype=jnp.int32).at[dest].set(flat_tok[order])
    row_gate = jnp.zeros((P,), jnp.float32).at[dest].set(gates.reshape(A)[order])
    block_exp = jnp.minimum(jnp.searchsorted(pad_end, jnp.arange(NB) * MOE_BLOCK, side='right'), N_EXPERTS - 1)
    h_pad = jnp.concatenate([hf, jnp.zeros((1, D), hf.dtype)], axis=0)

    def expert_block(args):
        rows, e = args
        xb = h_pad[rows]
        hu = xb @ w1[e] + b1[e]
        gate = jnp.minimum(hu[:, :D_FF], SWIGLU_LIMIT)
        up = jnp.clip(hu[:, D_FF:], -SWIGLU_LIMIT, SWIGLU_LIMIT)
        act = (up + 1.0) * gate * jax.nn.sigmoid(SWIGLU_ALPHA * gate)
        return act @ w2[e] + b2[e]

    yb = lax.map(expert_block, (row_tok.reshape(NB, MOE_BLOCK), block_exp))
    y = jnp.zeros((N + 1, D), jnp.float32).at[row_tok].add(yb.reshape(P, D).astype(jnp.float32) * row_gate[:, None])
    return y[:N].reshape(B, S, D).astype(h.dtype)


def layer(x, h0_re, h0_im, p):
    h = rmsnorm(x, p['norm1_g'])
    z = h @ p['w_in']
    z_u = z[..., :GM_WIDTH]
    z_v = z[..., GM_WIDTH:2 * GM_WIDTH]
    z_s = z[..., 2 * GM_WIDTH:]
    ya, v_rows = gmlp_mixer(z_u, z_v, p['gm_ln_g'], p['gm_ln_b'], p['gm_ws'], p['gm_bs'])
    yb, hr, hi = s5_mixer(z_s, h0_re, h0_im, p['ssm_lambda_re'], p['ssm_lambda_im'], p['ssm_log_dt'],
                          p['ssm_b_re'], p['ssm_b_im'], p['ssm_c_re'], p['ssm_c_im'], p['ssm_d'],
                          p['ssm_glu_w'], p['ssm_glu_b'])
    merged = jnp.concatenate([rmsnorm(ya, p['mix_norm_a_g']), rmsnorm(yb, p['mix_norm_b_g'])], axis=-1)
    x = x + merged @ p['w_out']
    x = x + moe(rmsnorm(x, p['norm2_g']), p['router_w'], p['router_b'],
                p['moe_w1'], p['moe_b1'], p['moe_w2'], p['moe_b2'])
    return x, v_rows, hr, hi


def setup_inputs(seed: int = 0) -> dict:
    key = jax.random.key(seed)
    ks = jax.random.split(key, 32)
    f32 = jnp.float32
    nrm = lambda k, shape, s: (jax.random.normal(k, shape, f32) * s).astype(f32)
    n_idx = jnp.arange(SSM_STATE, dtype=f32)
    return {
        'x_prompt': nrm(ks[0], (BATCH, SEQ, D_MODEL), 1.0),
        'x_sample': nrm(ks[1], (DEC_BATCH, DEC_SEQ, D_MODEL), 1.0),
        'state_ssm_re': nrm(ks[2], (DEPTH, DEC_BATCH, SSM_GROUPS, SSM_STATE), 0.1),
        'state_ssm_im': nrm(ks[3], (DEPTH, DEC_BATCH, SSM_GROUPS, SSM_STATE), 0.1),
        'norm1_g': 1.0 + nrm(ks[4], (DEPTH, D_MODEL), 0.02),
        'w_in': nrm(ks[5], (DEPTH, D_MODEL, 2 * GM_WIDTH + SSM_WIDTH), D_MODEL ** -0.5),
        'gm_ln_g': 1.0 + nrm(ks[6], (DEPTH, GM_WIDTH), 0.02),
        'gm_ln_b': nrm(ks[7], (DEPTH, GM_WIDTH), 0.02),
        'gm_ws': nrm(ks[8], (DEPTH, GM_HEADS, GM_CHUNK, GM_CHUNK), 0.5 * GM_CHUNK ** -0.5),
        'gm_bs': 1.0 + nrm(ks[9], (DEPTH, GM_HEADS, GM_CHUNK), 0.02),
        'ssm_lambda_re': -0.5 + nrm(ks[10], (DEPTH, SSM_GROUPS, SSM_STATE), 0.01),
        'ssm_lambda_im': jnp.pi * n_idx + nrm(ks[11], (DEPTH, SSM_GROUPS, SSM_STATE), 0.01),
        'ssm_log_dt': jax.random.uniform(ks[12], (DEPTH, SSM_GROUPS), f32, np.log(DT_MIN), np.log(DT_MAX)),
        'ssm_b_re': nrm(ks[13], (DEPTH, SSM_GROUPS, SSM_STATE, SSM_GROUP), (2.0 * SSM_GROUP) ** -0.5),
        'ssm_b_im': nrm(ks[14], (DEPTH, SSM_GROUPS, SSM_STATE, SSM_GROUP), (2.0 * SSM_GROUP) ** -0.5),
        'ssm_c_re': nrm(ks[15], (DEPTH, SSM_GROUPS, SSM_GROUP, SSM_STATE), (2.0 * SSM_STATE) ** -0.5),
        'ssm_c_im': nrm(ks[16], (DEPTH, SSM_GROUPS, SSM_GROUP, SSM_STATE), (2.0 * SSM_STATE) ** -0.5),
        'ssm_d': nrm(ks[17], (DEPTH, SSM_WIDTH), 1.0),
        'ssm_glu_w': nrm(ks[18], (DEPTH, SSM_WIDTH, SSM_WIDTH), SSM_WIDTH ** -0.5),
        'ssm_glu_b': nrm(ks[19], (DEPTH, SSM_WIDTH), 0.02),
        'mix_norm_a_g': 1.0 + nrm(ks[20], (DEPTH, GM_WIDTH), 0.02),
        'mix_norm_b_g': 1.0 + nrm(ks[21], (DEPTH, SSM_WIDTH), 0.02),
        'w_out': nrm(ks[22], (DEPTH, D_MIX, D_MODEL), D_MIX ** -0.5),
        'norm2_g': 1.0 + nrm(ks[23], (DEPTH, D_MODEL), 0.02),
        'router_w': nrm(ks[24], (DEPTH, D_MODEL, N_EXPERTS), D_MODEL ** -0.5),
        'router_b': nrm(ks[25], (DEPTH, N_EXPERTS), 0.01),
        'moe_w1': nrm(ks[26], (DEPTH, N_EXPERTS, D_MODEL, 2 * D_FF), D_MODEL ** -0.5),
        'moe_b1': nrm(ks[27], (DEPTH, N_EXPERTS, 2 * D_FF), 0.02),
        'moe_w2': nrm(ks[28], (DEPTH, N_EXPERTS, D_FF, D_MODEL), D_FF ** -0.5),
        'moe_b2': nrm(ks[29], (DEPTH, N_EXPERTS, D_MODEL), 0.02),
        'final_norm_g': 1.0 + nrm(ks[30], (D_MODEL,), 0.02),
    }


def reference(x_prompt, x_sample, state_ssm_re, state_ssm_im, norm1_g, w_in, gm_ln_g, gm_ln_b, gm_ws, gm_bs,
              ssm_lambda_re, ssm_lambda_im, ssm_log_dt, ssm_b_re, ssm_b_im, ssm_c_re, ssm_c_im, ssm_d,
              ssm_glu_w, ssm_glu_b, mix_norm_a_g, mix_norm_b_g, w_out, norm2_g, router_w, router_b,
              moe_w1, moe_b1, moe_w2, moe_b2, final_norm_g):
    xp = x_prompt
    xs = x_sample
    bp = x_prompt.shape[0]
    zeros_state = jnp.zeros((bp, SSM_GROUPS, SSM_STATE), jnp.float32)
    re_p, im_p, re_s, im_s, v_s = [], [], [], [], []
    for l in range(DEPTH):
        p = {
            'norm1_g': norm1_g[l], 'w_in': w_in[l], 'gm_ln_g': gm_ln_g[l], 'gm_ln_b': gm_ln_b[l],
            'gm_ws': gm_ws[l], 'gm_bs': gm_bs[l], 'ssm_lambda_re': ssm_lambda_re[l],
            'ssm_lambda_im': ssm_lambda_im[l], 'ssm_log_dt': ssm_log_dt[l], 'ssm_b_re': ssm_b_re[l],
            'ssm_b_im': ssm_b_im[l], 'ssm_c_re': ssm_c_re[l], 'ssm_c_im': ssm_c_im[l], 'ssm_d': ssm_d[l],
            'ssm_glu_w': ssm_glu_w[l], 'ssm_glu_b': ssm_glu_b[l], 'mix_norm_a_g': mix_norm_a_g[l],
            'mix_norm_b_g': mix_norm_b_g[l], 'w_out': w_out[l], 'norm2_g': norm2_g[l],
            'router_w': router_w[l], 'router_b': router_b[l], 'moe_w1': moe_w1[l], 'moe_b1': moe_b1[l],
            'moe_w2': moe_w2[l], 'moe_b2': moe_b2[l],
        }
        xp, _, hr_p, hi_p = layer(xp, zeros_state, zeros_state, p)
        xs, vrows_s, hr_s, hi_s = layer(xs, state_ssm_re[l], state_ssm_im[l], p)
        re_p.append(hr_p)
        im_p.append(hi_p)
        re_s.append(hr_s)
        im_s.append(hi_s)
        v_s.append(vrows_s)
    y_prompt = rmsnorm(xp, final_norm_g)
    y_sample = rmsnorm(xs, final_norm_g)
    return (y_prompt, y_sample, jnp.stack(re_p), jnp.stack(im_p), jnp.stack(re_s), jnp.stack(im_s), jnp.stack(v_s))
```

```python
import functools

import jax
import jax.numpy as jnp
from jax import lax
from jax.experimental import pallas as pl
from jax.experimental.pallas import tpu as pltpu

F32 = jnp.float32
BF16 = jnp.bfloat16

D_MODEL = 2048
GM_WIDTH = 1024
GM_HEADS = 8
GM_HEAD_DIM = 128
GM_CHUNK = 128
CHUNK = 64
SSM_WIDTH = 1024
SSM_GROUP = 16
SSM_GROUPS = 64
SSM_STATE = 64
SSM_FLAT = SSM_GROUPS * SSM_STATE
N_EXPERTS = 32
TOP_K = 4
D_FF = 2048
SWIGLU_LIMIT = 7.0
SWIGLU_ALPHA = 1.702
NORM_EPS = 1e-5

LANES = 128
SUBLANES = 8
VMEM_LIMIT = 56 * 1024 * 1024

TOKEN_TILE = 512
S5_TIME_TILE = 64
SCAN_LANES = 1024
MOE_ROWS = 512
MOE_FF_TILE = 512
COMBINE_TILE = 128


def _gelu(x):
    return 0.5 * x * (1.0 + jnp.tanh(0.7978845608028654 * (x + 0.044715 * (x * x * x))))


def _sigmoid(x):
    return 1.0 / (1.0 + jnp.exp(-x))


def _rms_scale(x):
    return lax.rsqrt(jnp.mean(x * x, axis=-1, keepdims=True) + NORM_EPS)


def _params(*sem):
    return pltpu.CompilerParams(dimension_semantics=sem, vmem_limit_bytes=VMEM_LIMIT)


def _const_spec(shape):
    nd = len(shape)
    return pl.BlockSpec(shape, lambda *_: (0,) * nd)


def _in_proj_kernel(x_ref, g1_ref, win_ref, lng_ref, lnb_ref, wmix_ref, bsb_ref, ga_ref,
                    ya_ref, zs_ref, *rest, tm, emit_v):
    if emit_v:
        v_ref, ya_scr = rest
    else:
        (ya_scr,) = rest
    x = x_ref[...]
    h = (x * _rms_scale(x) * g1_ref[...]).astype(BF16)
    z = jnp.dot(h, win_ref[...], preferred_element_type=F32)
    zs_ref[...] = z[:, 2 * GM_WIDTH:]
    u = _gelu(z[:, :GM_WIDTH])
    gv = _gelu(z[:, GM_WIDTH:2 * GM_WIDTH])
    for hd in range(GM_HEADS):
        sl = slice(hd * GM_HEAD_DIM, (hd + 1) * GM_HEAD_DIM)
        g = gv[:, sl]
        d = g - jnp.mean(g, axis=-1, keepdims=True)
        var = jnp.mean(d * d, axis=-1, keepdims=True)
        vh = d * lax.rsqrt(var + NORM_EPS) * lng_ref[:, sl] + lnb_ref[:, sl]
        if emit_v:
            v_ref[:, sl] = vh
        vb = vh.astype(BF16)
        for c in range(tm // GM_CHUNK):
            rows = slice(c * GM_CHUNK, (c + 1) * GM_CHUNK)
            s = jnp.dot(wmix_ref[hd], vb[rows], preferred_element_type=F32) + bsb_ref[:, sl]
            ya_scr[rows, sl] = u[rows, sl] * s
    ya = ya_scr[...]
    ya_ref[...] = (ya * _rms_scale(ya) * ga_ref[...]).astype(BF16)


def _in_proj(x2d, g1, win16, lng, lnb, wmix16, bsb, ga, *, emit_v):
    n = x2d.shape[0]
    tm = min(TOKEN_TILE, n)
    assert n % tm == 0 and tm % GM_CHUNK == 0
    row = lambda w: pl.BlockSpec((tm, w), lambda i: (i, 0))
    out_shape = [jax.ShapeDtypeStruct((n, GM_WIDTH), BF16), jax.ShapeDtypeStruct((n, SSM_WIDTH), F32)]
    out_specs = [row(GM_WIDTH), row(SSM_WIDTH)]
    if emit_v:
        out_shape.append(jax.ShapeDtypeStruct((n, GM_WIDTH), F32))
        out_specs.append(row(GM_WIDTH))
    return pl.pallas_call(
        functools.partial(_in_proj_kernel, tm=tm, emit_v=emit_v),
        grid=(n // tm,),
        in_specs=[row(D_MODEL), _const_spec((1, D_MODEL)), _const_spec(win16.shape),
                  _const_spec((1, GM_WIDTH)), _const_spec((1, GM_WIDTH)), _const_spec(wmix16.shape),
                  _const_spec(bsb.shape), _const_spec((1, GM_WIDTH))],
        out_specs=out_specs,
        out_shape=out_shape,
        scratch_shapes=[pltpu.VMEM((tm, GM_WIDTH), F32)],
        compiler_params=_params("arbitrary"),
        name="in_proj",
    )(x2d, g1, win16, lng, lnb, wmix16, bsb, ga)


def _s5_kernel(zs_ref, h0r_ref, h0i_ref, ar_ref, ai_ref, wbr_ref, wbi_ref, cr_ref, ci_ref,
               dsk_ref, gluw_ref, glub_ref, gb_ref,
               yb_ref, hr_out, hi_out,
               ztb, hre, him, ytb, car_r, car_i, *, tt):
    nseq = SUBLANES
    ngrp = SSM_WIDTH // LANES
    wcol = SSM_FLAT // ngrp
    for j in range(ngrp):
        for b in range(nseq):
            ztb[j, pl.ds(b, tt, stride=nseq), :] = zs_ref[b, :, j * LANES:(j + 1) * LANES]
        lhs = ztb[j].astype(BF16)
        hre[:, j * wcol:(j + 1) * wcol] = jnp.dot(lhs, wbr_ref[j], preferred_element_type=F32)
        him[:, j * wcol:(j + 1) * wcol] = jnp.dot(lhs, wbi_ref[j], preferred_element_type=F32)

    @pl.when(pl.program_id(1) == 0)
    def _():
        car_r[...] = h0r_ref[...]
        car_i[...] = h0i_ref[...]

    for lc in range(SSM_FLAT // SCAN_LANES):
        sl = slice(lc * SCAN_LANES, (lc + 1) * SCAN_LANES)
        a_re = ar_ref[:, sl]
        a_im = ai_ref[:, sl]

        def step(t, carry, sl=sl, a_re=a_re, a_im=a_im):
            s_re, s_im = carry
            rows = pl.ds(pl.multiple_of(t * nseq, nseq), nseq)
            n_re = a_re * s_re - a_im * s_im + hre[rows, sl]
            n_im = a_re * s_im + a_im * s_re + him[rows, sl]
            hre[rows, sl] = n_re
            him[rows, sl] = n_im
            return n_re, n_im

        s_re, s_im = lax.fori_loop(0, tt, step, (car_r[:, sl], car_i[:, sl]), unroll=4)
        car_r[:, sl] = s_re
        car_i[:, sl] = s_im
    hr_out[...] = car_r[...]
    hi_out[...] = car_i[...]

    cols = []
    for q in range(ngrp):
        xr = hre[:, q * wcol:(q + 1) * wcol].astype(BF16)
        xi = him[:, q * wcol:(q + 1) * wcol].astype(BF16)
        yq = (jnp.dot(xr, cr_ref[q], preferred_element_type=F32)
              - jnp.dot(xi, ci_ref[q], preferred_element_type=F32))
        cols.append(_gelu(yq + dsk_ref[:, q * LANES:(q + 1) * LANES] * ztb[q]))
    y = jnp.concatenate(cols, axis=-1)
    gl = jnp.dot(y.astype(BF16), gluw_ref[...], preferred_element_type=F32) + glub_ref[...]
    y = y * _sigmoid(gl)
    y = y * _rms_scale(y) * gb_ref[...]
    for q in range(ngrp):
        ytb[q] = y[:, q * LANES:(q + 1) * LANES]
    for q in range(ngrp):
        for b in range(nseq):
            yb_ref[b, :, q * LANES:(q + 1) * LANES] = ytb[q, pl.ds(b, tt, stride=nseq), :].astype(BF16)


def _s5(zs3, h0r, h0i, sp):
    nb, t, _ = zs3.shape
    tt = min(S5_TIME_TILE, t)
    assert nb % SUBLANES == 0 and t % tt == 0 and tt % 16 == 0
    rows = SUBLANES * tt
    seq_spec = pl.BlockSpec((SUBLANES, tt, SSM_WIDTH), lambda g, i: (g, i, 0))
    st_spec = pl.BlockSpec((SUBLANES, SSM_FLAT), lambda g, i: (g, 0))
    consts = [sp["ar"], sp["ai"], sp["wbr"], sp["wbi"], sp["cr"], sp["ci"], sp["dsk"], sp["gluw"],
              sp["glub"], sp["gb"]]
    return pl.pallas_call(
        functools.partial(_s5_kernel, tt=tt),
        grid=(nb // SUBLANES, t // tt),
        in_specs=[seq_spec, st_spec, st_spec] + [_const_spec(c.shape) for c in consts],
        out_specs=[seq_spec, st_spec, st_spec],
        out_shape=[jax.ShapeDtypeStruct((nb, t, SSM_WIDTH), BF16),
                   jax.ShapeDtypeStruct((nb, SSM_FLAT), F32),
                   jax.ShapeDtypeStruct((nb, SSM_FLAT), F32)],
        scratch_shapes=[pltpu.VMEM((SSM_WIDTH // LANES, rows, LANES), F32),
                        pltpu.VMEM((rows, SSM_FLAT), F32),
                        pltpu.VMEM((rows, SSM_FLAT), F32),
                        pltpu.VMEM((SSM_WIDTH // LANES, rows, LANES), F32),
                        pltpu.VMEM((SUBLANES, SSM_FLAT), F32),
                        pltpu.VMEM((SUBLANES, SSM_FLAT), F32)],
        compiler_params=_params("arbitrary", "arbitrary"),
        name="s5",
    )(zs3, h0r, h0i, *consts)


def _out_proj_kernel(xp_ref, yap_ref, ybp_ref, xs_ref, yas_ref, ybs_ref, wo_ref, g2_ref, rwh_ref, rwl_ref,
                     rb_ref, tri_ref, x1_ref, h2_ref, ridx_ref, rgate_ref, cnt_out_ref, cnt_scr, *, tm, n_p_tiles):
    i = pl.program_id(0)

    @pl.when(i == 0)
    def _():
        cnt_scr[...] = jnp.zeros_like(cnt_scr)

    def body(x_ref, ya_ref, yb_ref):
        y = (jnp.dot(ya_ref[...], wo_ref[:GM_WIDTH, :], preferred_element_type=F32)
             + jnp.dot(yb_ref[...], wo_ref[GM_WIDTH:, :], preferred_element_type=F32))
        x1 = x_ref[...] + y
        x1_ref[...] = x1
        h2 = x1 * _rms_scale(x1) * g2_ref[...]
        h2_ref[...] = h2

        hh = h2.astype(BF16)
        hl = (h2 - hh.astype(F32)).astype(BF16)
        lg = (jnp.dot(hh, rwh_ref[...], preferred_element_type=F32)
              + jnp.dot(hl, rwh_ref[...], preferred_element_type=F32)
              + jnp.dot(hh, rwl_ref[...], preferred_element_type=F32)) + rb_ref[...]

        lane = lax.broadcasted_iota(jnp.int32, (tm, LANES), 1)
        work = lg
        sel, val = [], []
        for _ in range(TOP_K):
            mx = jnp.max(work, axis=-1, keepdims=True)
            am = jnp.min(jnp.where(work == mx, lane, LANES), axis=-1, keepdims=True)
            sel.append(am)
            val.append(mx)
            work = jnp.where(lane == am, -jnp.inf, work)
        ex = [jnp.exp(v - val[0]) for v in val]
        inv = 1.0 / (ex[0] + ex[1] + ex[2] + ex[3])

        chosen = jnp.zeros((tm, LANES), F32)
        for am in sel:
            chosen = chosen + jnp.where(lane == am, 1.0, 0.0)
        before = jnp.dot(tri_ref[...], chosen.astype(BF16), preferred_element_type=F32) + cnt_scr[...]
        ridx = jnp.zeros((tm, LANES), F32)
        rgate = jnp.zeros((tm, LANES), F32)
        for k in range(TOP_K):
            rank = jnp.sum(jnp.where(lane == sel[k], before, 0.0), axis=-1, keepdims=True)
            ridx = (ridx + jnp.where(lane == k, sel[k].astype(F32), 0.0)
                    + jnp.where(lane == TOP_K + k, rank, 0.0))
            rgate = rgate + jnp.where(lane == k, ex[k] * inv, 0.0)
        ridx_ref[...] = ridx.astype(jnp.int32)
        rgate_ref[...] = rgate
        cnt_scr[...] = cnt_scr[...] + jnp.sum(chosen, axis=0, keepdims=True)
        cnt_out_ref[...] = cnt_scr[...]

    @pl.when(i < n_p_tiles)
    def _():
        body(xp_ref, yap_ref, ybp_ref)

    @pl.when(i >= n_p_tiles)
    def _():
        body(xs_ref, yas_ref, ybs_ref)


def _out_proj(prompt, sample, wo16, g2, rwh, rwl, rb, tri):
    tm = tri.shape[0]
    n_p, n_s = prompt[0].shape[0], sample[0].shape[0]
    assert n_p % tm == 0 and n_s % tm == 0
    n_p_tiles = n_p // tm
    n = n_p + n_s
    p_spec = lambda w: pl.BlockSpec((tm, w), lambda i: (jnp.minimum(i, n_p_tiles - 1), 0))
    s_spec = lambda w: pl.BlockSpec((tm, w), lambda i: (jnp.maximum(i - n_p_tiles, 0), 0))
    row = lambda w: pl.BlockSpec((tm, w), lambda i: (i, 0))
    widths = (D_MODEL, GM_WIDTH, SSM_WIDTH)
    return pl.pallas_call(
        functools.partial(_out_proj_kernel, tm=tm, n_p_tiles=n_p_tiles),
        grid=(n // tm,),
        in_specs=[p_spec(w) for w in widths] + [s_spec(w) for w in widths] + [
            _const_spec(wo16.shape), _const_spec((1, D_MODEL)), _const_spec(rwh.shape), _const_spec(rwl.shape),
            _const_spec((1, LANES)), _const_spec(tri.shape)],
        out_specs=[row(D_MODEL), row(D_MODEL), row(LANES), row(LANES), _const_spec((1, LANES))],
        out_shape=[jax.ShapeDtypeStruct((n, D_MODEL), F32), jax.ShapeDtypeStruct((n, D_MODEL), F32),
                   jax.ShapeDtypeStruct((n, LANES), jnp.int32), jax.ShapeDtypeStruct((n, LANES), F32),
                   jax.ShapeDtypeStruct((1, LANES), F32)],
        scratch_shapes=[pltpu.VMEM((1, LANES), F32)],
        compiler_params=_params("arbitrary"),
        name="out_proj",
    )(*prompt, *sample, wo16, g2, rwh, rwl, rb, tri)


def _moe_kernel(bexp_ref, nval_ref, nused_ref, tok_ref, slot_ref, h2_hbm, w1g_ref, w1u_ref, b1g_ref, b1u_ref,
                w2_ref, b2_ref, y4_hbm, xbuf, xb16, acc, gsem, ssem, *, tm, nj):
    del bexp_ref
    b = pl.program_id(0)
    j = pl.program_id(1)
    nv = nval_ref[b]

    def row_in(r):
        return pltpu.make_async_copy(h2_hbm.at[pl.ds(tok_ref[0, 0, r], 1), :], xbuf.at[pl.ds(r, 1), :], gsem)

    def row_out(r):
        return pltpu.make_async_copy(acc.at[pl.ds(r, 1), :], y4_hbm.at[pl.ds(slot_ref[0, 0, r], 1), :], ssem)

    @pl.when((b == 0) & (j == 0))
    def _():
        xbuf[...] = jnp.zeros_like(xbuf)

    @pl.when(b < nused_ref[0])
    def _():
        @pl.when(j == 0)
        def _():
            lax.fori_loop(0, nv, lambda r, c: (row_in(r).start(), c)[1], 0)
            lax.fori_loop(0, nv, lambda r, c: (row_in(r).wait(), c)[1], 0)
            rid = lax.broadcasted_iota(jnp.int32, (tm, 1), 0)
            xb16[...] = jnp.where(rid < nv, xbuf[...], 0.0).astype(BF16)
            acc[...] = jnp.zeros_like(acc)

        xb = xb16[...]
        hg = jnp.dot(xb, w1g_ref[0].astype(BF16), preferred_element_type=F32) + b1g_ref[0]
        hu = jnp.dot(xb, w1u_ref[0].astype(BF16), preferred_element_type=F32) + b1u_ref[0]
        gate = jnp.minimum(hg, SWIGLU_LIMIT)
        up = jnp.clip(hu, -SWIGLU_LIMIT, SWIGLU_LIMIT)
        act = (up + 1.0) * gate * _sigmoid(SWIGLU_ALPHA * gate)
        acc[...] += jnp.dot(act.astype(BF16), w2_ref[0].astype(BF16), preferred_element_type=F32)

        @pl.when(j == nj - 1)
        def _():
            acc[...] += b2_ref[0]
            lax.fori_loop(0, nv, lambda r, c: (row_out(r).start(), c)[1], 0)
            lax.fori_loop(0, nv, lambda r, c: (row_out(r).wait(), c)[1], 0)


def _moe(h2, w1, b1, w2, b2, bexp, nval, nused, toks, slots, *, n_rows_out):
    nb, _, tm = slots.shape
    tf = MOE_FF_TILE
    nj = D_FF // tf
    live = lambda b, nu: b < nu[0]
    jj = lambda b, j, nu: jnp.where(live(b, nu), j, nj - 1)
    grid_spec = pltpu.PrefetchScalarGridSpec(
        num_scalar_prefetch=3,
        grid=(nb, nj),
        in_specs=[
            pl.BlockSpec((1, 1, tm), lambda b, j, be, nv, nu: (b, 0, 0), memory_space=pltpu.SMEM),
            pl.BlockSpec((1, 1, tm), lambda b, j, be, nv, nu: (b, 0, 0), memory_space=pltpu.SMEM),
            pl.BlockSpec(memory_space=pl.ANY),
            pl.BlockSpec((1, D_MODEL, tf), lambda b, j, be, nv, nu: (be[b], 0, jj(b, j, nu))),
            pl.BlockSpec((1, D_MODEL, tf), lambda b, j, be, nv, nu: (be[b], 0, nj + jj(b, j, nu))),
            pl.BlockSpec((1, 1, tf), lambda b, j, be, nv, nu: (be[b], 0, jj(b, j, nu))),
            pl.BlockSpec((1, 1, tf), lambda b, j, be, nv, nu: (be[b], 0, nj + jj(b, j, nu))),
            pl.BlockSpec((1, tf, D_MODEL), lambda b, j, be, nv, nu: (be[b], jj(b, j, nu), 0)),
            pl.BlockSpec((1, 1, D_MODEL), lambda b, j, be, nv, nu: (be[b], 0, 0)),
        ],
        out_specs=pl.BlockSpec(memory_space=pl.ANY),
        scratch_shapes=[pltpu.VMEM((tm, D_MODEL), F32), pltpu.VMEM((tm, D_MODEL), BF16),
                        pltpu.VMEM((tm, D_MODEL), F32), pltpu.SemaphoreType.DMA, pltpu.SemaphoreType.DMA],
    )
    return pl.pallas_call(
        functools.partial(_moe_kernel, tm=tm, nj=nj),
        grid_spec=grid_spec,
        out_shape=jax.ShapeDtypeStruct((n_rows_out, D_MODEL), F32),
        compiler_params=_params("arbitrary", "arbitrary"),
        name="moe",
    )(bexp, nval, nused, toks, slots, h2, w1, w1, b1, b1, w2, b2)


def _combine_kernel(x1_ref, y4_ref, gate_ref, gf_ref, out_ref):
    x = x1_ref[...]
    g = gate_ref[...]
    for k in range(TOP_K):
        x = x + g[:, k:k + 1] * y4_ref[k]
    out_ref[...] = x * _rms_scale(x) * gf_ref[...]


def _combine(x1, y4w, gates, gf, *, row_off, n_rows):
    tc = min(COMBINE_TILE, n_rows)
    assert n_rows % tc == 0 and row_off % tc == 0
    off = row_off // tc
    return pl.pallas_call(
        _combine_kernel,
        grid=(n_rows // tc,),
        in_specs=[pl.BlockSpec((tc, D_MODEL), lambda i: (i + off, 0)),
                  pl.BlockSpec((TOP_K, tc, D_MODEL), lambda i: (0, i + off, 0)),
                  pl.BlockSpec((tc, LANES), lambda i: (i + off, 0)),
                  _const_spec((1, D_MODEL))],
        out_specs=pl.BlockSpec((tc, D_MODEL), lambda i: (i, 0)),
        out_shape=jax.ShapeDtypeStruct((n_rows, D_MODEL), F32),
        compiler_params=_params("arbitrary"),
        name="combine",
    )(x1, y4w, gates, gf)


def _mix_tables(ws, bs, t):
    pos = jnp.arange(GM_CHUNK)
    mask = (pos[None, :] // CHUNK) <= (pos[:, None] // CHUNK)
    wm = jnp.where(mask[None], ws, 0.0)
    if t % GM_CHUNK == 0:
        bias_rows = bs
    else:
        assert GM_CHUNK % t == 0
        rep = GM_CHUNK // t
        eye = jnp.eye(rep, dtype=ws.dtype)
        wm = jnp.einsum("ab,hij->haibj", eye, wm[:, :t, :t]).reshape(GM_HEADS, GM_CHUNK, GM_CHUNK)
        bias_rows = jnp.tile(bs[:, :t], (1, rep))
    bsb = jnp.repeat(bias_rows.T, GM_HEAD_DIM, axis=1)
    return wm.astype(BF16), bsb.astype(F32)


def _s5_tables(lam_re, lam_im, log_dt, b_re, b_im, c_re, c_im, d_skip, glu_w, glu_b, gain_b):
    lr = lam_re.astype(F32)
    li = lam_im.astype(F32)
    dt = jnp.exp(log_dt.astype(F32))[:, None]
    mag = jnp.exp(lr * dt)
    ab_re = mag * jnp.cos(li * dt)
    ab_im = mag * jnp.sin(li * dt)
    den = lr * lr + li * li
    f_re = ((ab_re - 1.0) * lr + ab_im * li) / den
    f_im = (ab_im * lr - (ab_re - 1.0) * li) / den
    br = b_re.astype(F32)
    bi = b_im.astype(F32)
    bb_re = f_re[..., None] * br - f_im[..., None] * bi
    bb_im = f_re[..., None] * bi + f_im[..., None] * br
    per = LANES // SSM_GROUP
    nt = SSM_GROUPS // per
    eye = jnp.eye(per, dtype=F32)

    def in_blocks(bb):
        return jnp.einsum("jgpc,gh->jgchp", bb.reshape(nt, per, SSM_STATE, SSM_GROUP), eye).reshape(
            nt, LANES, per * SSM_STATE).astype(BF16)

    def out_blocks(cc):
        return jnp.einsum("qgcp,gh->qgphc", cc.astype(F32).reshape(nt, per, SSM_GROUP, SSM_STATE), eye).reshape(
            nt, per * SSM_STATE, LANES).astype(BF16)

    bcast = lambda a: jnp.broadcast_to(a.reshape(1, SSM_FLAT), (SUBLANES, SSM_FLAT))
    return dict(ar=bcast(ab_re), ai=bcast(ab_im), wbr=in_blocks(bb_re), wbi=in_blocks(bb_im),
                cr=out_blocks(c_re), ci=out_blocks(c_im), dsk=d_skip.astype(F32).reshape(1, SSM_WIDTH),
                gluw=glu_w.astype(BF16), glub=glu_b.astype(F32).reshape(1, SSM_WIDTH),
                gb=gain_b.astype(F32).reshape(1, SSM_WIDTH))


def _routing_tables(ridx, counts, tm, nb):
    n = ridx.shape[0]
    eidx = ridx[:, :TOP_K]
    rank = ridx[:, TOP_K:2 * TOP_K]
    padded = (counts + tm - 1) // tm * tm
    pad_end = jnp.cumsum(padded)
    pad_start = pad_end - padded
    pos = pad_start[eidx] + rank
    ids = jnp.arange(n * TOP_K, dtype=jnp.int32)
    ids = jnp.zeros((nb * tm,), jnp.int32).at[pos.reshape(-1)].set(ids, unique_indices=True)
    toks = lax.shift_right_logical(ids, 2)
    slots = (ids & (TOP_K - 1)) * n + toks
    nused = (pad_end[-1] // tm).astype(jnp.int32)
    blk = jnp.arange(nb, dtype=jnp.int32)
    bexp = jnp.minimum(jnp.searchsorted(pad_end, jnp.minimum(blk, nused - 1) * tm, side="right"),
                       N_EXPERTS - 1).astype(jnp.int32)
    nval = jnp.where(blk < nused, jnp.clip(counts[bexp] - (blk * tm - pad_start[bexp]), 0, tm), 0)
    return bexp, nval.astype(jnp.int32), nused.reshape(1), toks.reshape(nb, 1, tm), slots.reshape(nb, 1, tm)


@jax.jit
def kernel(x_prompt, x_sample, state_ssm_re, state_ssm_im, norm1_g, w_in, gm_ln_g, gm_ln_b, gm_ws, gm_bs, ssm_lambda_re, ssm_lambda_im, ssm_log_dt, ssm_b_re, ssm_b_im, ssm_c_re, ssm_c_im, ssm_d, ssm_glu_w, ssm_glu_b, mix_norm_a_g, mix_norm_b_g, w_out, norm2_g, router_w, router_b, moe_w1, moe_b1, moe_w2, moe_b2, final_norm_g):
    assert w_in.shape[0] == 1, "single-layer model"
    bp, sp_len, _ = x_prompt.shape
    bs_, ss_len, _ = x_sample.shape
    n_p, n_s = bp * sp_len, bs_ * ss_len
    n = n_p + n_s
    row = lambda a, w: a.astype(F32).reshape(1, w)

    win16 = w_in[0].astype(BF16)
    wo16 = w_out[0].astype(BF16)
    s5p = _s5_tables(ssm_lambda_re[0], ssm_lambda_im[0], ssm_log_dt[0], ssm_b_re[0], ssm_b_im[0], ssm_c_re[0],
                     ssm_c_im[0], ssm_d[0], ssm_glu_w[0], ssm_glu_b[0], mix_norm_b_g[0])
    rw = jnp.pad(router_w[0].astype(F32), ((0, 0), (0, LANES - N_EXPERTS)))
    rwh = rw.astype(BF16)
    rwl = (rw - rwh.astype(F32)).astype(BF16)
    rb = jnp.pad(router_b[0].astype(F32), (0, LANES - N_EXPERTS), constant_values=-1e30).reshape(1, LANES)
    tm = min(TOKEN_TILE, n_s, n_p)
    tri = (jnp.arange(tm)[:, None] > jnp.arange(tm)[None, :]).astype(BF16)

    streams = []
    for x, t, h0 in ((x_prompt, sp_len, None), (x_sample, ss_len, (state_ssm_re[0], state_ssm_im[0]))):
        nb = x.shape[0]
        x2d = x.reshape(nb * t, D_MODEL)
        wmix16, bsb = _mix_tables(gm_ws[0], gm_bs[0], t)
        emit_v = h0 is not None
        outs = _in_proj(x2d, row(norm1_g[0], D_MODEL), win16, row(gm_ln_g[0], GM_WIDTH), row(gm_ln_b[0], GM_WIDTH),
                        wmix16, bsb, row(mix_norm_a_g[0], GM_WIDTH), emit_v=emit_v)
        ya, zs = outs[0], outs[1]
        v_rows = outs[2] if emit_v else None
        if h0 is None:
            h0r = h0i = jnp.zeros((nb, SSM_FLAT), F32)
        else:
            h0r = h0[0].astype(F32).reshape(nb, SSM_FLAT)
            h0i = h0[1].astype(F32).reshape(nb, SSM_FLAT)
        yb3, hr, hi = _s5(zs.reshape(nb, t, SSM_WIDTH), h0r, h0i, s5p)
        streams.append(dict(x2d=x2d, ya=ya, yb=yb3.reshape(nb * t, SSM_WIDTH), hr=hr, hi=hi, v=v_rows, nb=nb, t=t))

    sp_, ss_ = streams
    x1, h2, ridx, rgate, cnt = _out_proj((sp_["x2d"], sp_["ya"], sp_["yb"]), (ss_["x2d"], ss_["ya"], ss_["yb"]),
                                         wo16, row(norm2_g[0], D_MODEL), rwh, rwl, rb, tri)
    counts = cnt[0, :N_EXPERTS].astype(jnp.int32)
    nblk = -(-(n * TOP_K) // MOE_ROWS) + N_EXPERTS
    bexp, nval, nused, toks, slots = _routing_tables(ridx, counts, MOE_ROWS, nblk)
    y4 = _moe(h2, moe_w1[0], moe_b1[0].reshape(N_EXPERTS, 1, 2 * D_FF), moe_w2[0],
              moe_b2[0].reshape(N_EXPERTS, 1, D_MODEL), bexp, nval, nused, toks, slots, n_rows_out=n * TOP_K)
    y4w = y4.reshape(TOP_K, n, D_MODEL)

    outs = []
    for st, off in ((sp_, 0), (ss_, n_p)):
        y = _combine(x1, y4w, rgate, row(final_norm_g, D_MODEL), row_off=off, n_rows=st["nb"] * st["t"])
        outs.append(y.reshape(st["nb"], st["t"], D_MODEL))
    state = lambda a, nb: a.reshape(1, nb, SSM_GROUPS, SSM_STATE)
    return (outs[0], outs[1], state(sp_["hr"], bp), state(sp_["hi"], bp), state(ss_["hr"], bs_),
            state(ss_["hi"], bs_), ss_["v"].reshape(1, bs_, ss_len, GM_WIDTH))
```

```python
import functools

import jax
import jax.numpy as jnp
from jax import lax
from jax.experimental import pallas as pl
from jax.experimental.pallas import tpu as pltpu

F32 = jnp.float32
BF16 = jnp.bfloat16

D_MODEL = 2048
GM_WIDTH = 1024
GM_HEADS = 8
GM_HEAD_DIM = 128
GM_CHUNK = 128
CHUNK = 64
SSM_WIDTH = 1024
SSM_GROUP = 16
SSM_GROUPS = 64
SSM_STATE = 64
SSM_FLAT = SSM_GROUPS * SSM_STATE
N_EXPERTS = 32
TOP_K = 4
D_FF = 2048
SWIGLU_LIMIT = 7.0
SWIGLU_ALPHA = 1.702
NORM_EPS = 1e-5

LANES = 128
SUBLANES = 8
VMEM_LIMIT = 56 * 1024 * 1024

TOKEN_TILE = 512
S5_TIME_TILE = 64
SCAN_LANES = 1024
MOE_ROWS = 512
MOE_FF_TILE = 512
COMBINE_TILE = 128


def _gelu(x):
    return 0.5 * x * (1.0 + jnp.tanh(0.7978845608028654 * (x + 0.044715 * (x * x * x))))


def _sigmoid(x):
    return 1.0 / (1.0 + jnp.exp(-x))


def _rms_scale(x):
    return lax.rsqrt(jnp.mean(x * x, axis=-1, keepdims=True) + NORM_EPS)


def _params(*sem):
    return pltpu.CompilerParams(dimension_semantics=sem, vmem_limit_bytes=VMEM_LIMIT)


def _const_spec(shape):
    nd = len(shape)
    return pl.BlockSpec(shape, lambda *_: (0,) * nd)


def _in_proj_kernel(x_ref, g1_ref, win_ref, lng_ref, lnb_ref, wmix_ref, bsb_ref, ga_ref,
                    ya_ref, zs_ref, *rest, tm, emit_v):
    if emit_v:
        v_ref, ya_scr = rest
    else:
        (ya_scr,) = rest
    x = x_ref[...]
    h = (x * _rms_scale(x) * g1_ref[...]).astype(BF16)
    z = jnp.dot(h, win_ref[...], preferred_element_type=F32)
    zs_ref[...] = z[:, 2 * GM_WIDTH:]
    u = _gelu(z[:, :GM_WIDTH])
    gv = _gelu(z[:, GM_WIDTH:2 * GM_WIDTH])
    for hd in range(GM_HEADS):
        sl = slice(hd * GM_HEAD_DIM, (hd + 1) * GM_HEAD_DIM)
        g = gv[:, sl]
        d = g - jnp.mean(g, axis=-1, keepdims=True)
        var = jnp.mean(d * d, axis=-1, keepdims=True)
        vh = d * lax.rsqrt(var + NORM_EPS) * lng_ref[:, sl] + lnb_ref[:, sl]
        if emit_v:
            v_ref[:, sl] = vh
        vb = vh.astype(BF16)
        for c in range(tm // GM_CHUNK):
            rows = slice(c * GM_CHUNK, (c + 1) * GM_CHUNK)
            s = jnp.dot(wmix_ref[hd], vb[rows], preferred_element_type=F32) + bsb_ref[:, sl]
            ya_scr[rows, sl] = u[rows, sl] * s
    ya = ya_scr[...]
    ya_ref[...] = (ya * _rms_scale(ya) * ga_ref[...]).astype(BF16)


def _in_proj(x2d, g1, win16, lng, lnb, wmix16, bsb, ga, *, emit_v):
    n = x2d.shape[0]
    tm = min(TOKEN_TILE, n)
    assert n % tm == 0 and tm % GM_CHUNK == 0
    row = lambda w: pl.BlockSpec((tm, w), lambda i: (i, 0))
    out_shape = [jax.ShapeDtypeStruct((n, GM_WIDTH), BF16), jax.ShapeDtypeStruct((n, SSM_WIDTH), F32)]
    out_specs = [row(GM_WIDTH), row(SSM_WIDTH)]
    if emit_v:
        out_shape.append(jax.ShapeDtypeStruct((n, GM_WIDTH), F32))
        out_specs.append(row(GM_WIDTH))
    return pl.pallas_call(
        functools.partial(_in_proj_kernel, tm=tm, emit_v=emit_v),
        grid=(n // tm,),
        in_specs=[row(D_MODEL), _const_spec((1, D_MODEL)), _const_spec(win16.shape),
                  _const_spec((1, GM_WIDTH)), _const_spec((1, GM_WIDTH)), _const_spec(wmix16.shape),
                  _const_spec(bsb.shape), _const_spec((1, GM_WIDTH))],
        out_specs=out_specs,
        out_shape=out_shape,
        scratch_shapes=[pltpu.VMEM((tm, GM_WIDTH), F32)],
        compiler_params=_params("arbitrary"),
        name="in_proj",
    )(x2d, g1, win16, lng, lnb, wmix16, bsb, ga)


def _s5_kernel(zs_ref, h0r_ref, h0i_ref, ar_ref, ai_ref, wbr_ref, wbi_ref, cr_ref, ci_ref,
               dsk_ref, gluw_ref, glub_ref, gb_ref,
               yb_ref, hr_out, hi_out,
               ztb, hre, him, ytb, car_r, car_i, *, tt):
    nseq = SUBLANES
    ngrp = SSM_WIDTH // LANES
    wcol = SSM_FLAT // ngrp
    for j in range(ngrp):
        for b in range(nseq):
            ztb[j, pl.ds(b, tt, stride=nseq), :] = zs_ref[b, :, j * LANES:(j + 1) * LANES]
        lhs = ztb[j].astype(BF16)
        hre[:, j * wcol:(j + 1) * wcol] = jnp.dot(lhs, wbr_ref[j], preferred_element_type=F32)
        him[:, j * wcol:(j + 1) * wcol] = jnp.dot(lhs, wbi_ref[j], preferred_element_type=F32)

    @pl.when(pl.program_id(1) == 0)
    def _():
        car_r[...] = h0r_ref[...]
        car_i[...] = h0i_ref[...]

    for lc in range(SSM_FLAT // SCAN_LANES):
        sl = slice(lc * SCAN_LANES, (lc + 1) * SCAN_LANES)
        a_re = ar_ref[:, sl]
        a_im = ai_ref[:, sl]

        def step(t, carry, sl=sl, a_re=a_re, a_im=a_im):
            s_re, s_im = carry
            rows = pl.ds(pl.multiple_of(t * nseq, nseq), nseq)
            n_re = a_re * s_re - a_im * s_im + hre[rows, sl]
            n_im = a_re * s_im + a_im * s_re + him[rows, sl]
            hre[rows, sl] = n_re
            him[rows, sl] = n_im
            return n_re, n_im

        s_re, s_im = lax.fori_loop(0, tt, step, (car_r[:, sl], car_i[:, sl]), unroll=4)
        car_r[:, sl] = s_re
        car_i[:, sl] = s_im
    hr_out[...] = car_r[...]
    hi_out[...] = car_i[...]

    cols = []
    for q in range(ngrp):
        xr = hre[:, q * wcol:(q + 1) * wcol].astype(BF16)
        xi = him[:, q * wcol:(q + 1) * wcol].astype(BF16)
        yq = (jnp.dot(xr, cr_ref[q], preferred_element_type=F32)
              - jnp.dot(xi, ci_ref[q], preferred_element_type=F32))
        cols.append(_gelu(yq + dsk_ref[:, q * LANES:(q + 1) * LANES] * ztb[q]))
    y = jnp.concatenate(cols, axis=-1)
    gl = jnp.dot(y.astype(BF16), gluw_ref[...], preferred_element_type=F32) + glub_ref[...]
    y = y * _sigmoid(gl)
    y = y * _rms_scale(y) * gb_ref[...]
    for q in range(ngrp):
        ytb[q] = y[:, q * LANES:(q + 1) * LANES]
    for q in range(ngrp):
        for b in range(nseq):
            yb_ref[b, :, q * LANES:(q + 1) * LANES] = ytb[q, pl.ds(b, tt, stride=nseq), :].astype(BF16)


def _s5(zs3, h0r, h0i, sp):
    nb, t, _ = zs3.shape
    tt = min(S5_TIME_TILE, t)
    assert nb % SUBLANES == 0 and t % tt == 0 and tt % 16 == 0
    rows = SUBLANES * tt
    seq_spec = pl.BlockSpec((SUBLANES, tt, SSM_WIDTH), lambda g, i: (g, i, 0))
    st_spec = pl.BlockSpec((SUBLANES, SSM_FLAT), lambda g, i: (g, 0))
    consts = [sp["ar"], sp["ai"], sp["wbr"], sp["wbi"], sp["cr"], sp["ci"], sp["dsk"], sp["gluw"],
              sp["glub"], sp["gb"]]
    return pl.pallas_call(
        functools.partial(_s5_kernel, tt=tt),
        grid=(nb // SUBLANES, t // tt),
        in_specs=[seq_spec, st_spec, st_spec] + [_const_spec(c.shape) for c in consts],
        out_specs=[seq_spec, st_spec, st_spec],
        out_shape=[jax.ShapeDtypeStruct((nb, t, SSM_WIDTH), BF16),
                   jax.ShapeDtypeStruct((nb, SSM_FLAT), F32),
                   jax.ShapeDtypeStruct((nb, SSM_FLAT), F32)],
        scratch_shapes=[pltpu.VMEM((SSM_WIDTH // LANES, rows, LANES), F32),
                        pltpu.VMEM((rows, SSM_FLAT), F32),
                        pltpu.VMEM((rows, SSM_FLAT), F32),
                        pltpu.VMEM((SSM_WIDTH // LANES, rows, LANES), F32),
                        pltpu.VMEM((SUBLANES, SSM_FLAT), F32),
                        pltpu.VMEM((SUBLANES, SSM_FLAT), F32)],
        compiler_params=_params("arbitrary", "arbitrary"),
        name="s5",
    )(zs3, h0r, h0i, *consts)


def _out_proj_kernel(xp_ref, yap_ref, ybp_ref, xs_ref, yas_ref, ybs_ref, wo_ref, g2_ref, rwh_ref, rwl_ref,
                     rb_ref, tri_ref, x1_ref, h2_ref, ridx_ref, rgate_ref, cnt_out_ref, cnt_scr, *, tm, n_p_tiles):
    i = pl.program_id(0)

    @pl.when(i == 0)
    def _():
        cnt_scr[...] = jnp.zeros_like(cnt_scr)

    def body(x_ref, ya_ref, yb_ref):
        y = (jnp.dot(ya_ref[...], wo_ref[:GM_WIDTH, :], preferred_element_type=F32)
             + jnp.dot(yb_ref[...], wo_ref[GM_WIDTH:, :], preferred_element_type=F32))
        x1 = x_ref[...] + y
        x1_ref[...] = x1
        h2 = x1 * _rms_scale(x1) * g2_ref[...]
        h2_ref[...] = h2

        hh = h2.astype(BF16)
        hl = (h2 - hh.astype(F32)).astype(BF16)
        lg = (jnp.dot(hh, rwh_ref[...], preferred_element_type=F32)
              + jnp.dot(hl, rwh_ref[...], preferred_element_type=F32)
              + jnp.dot(hh, rwl_ref[...], preferred_element_type=F32)) + rb_ref[...]

        lane = lax.broadcasted_iota(jnp.int32, (tm, LANES), 1)
        work = lg
        sel, val = [], []
        for _ in range(TOP_K):
            mx = jnp.max(work, axis=-1, keepdims=True)
            am = jnp.min(jnp.where(work == mx, lane, LANES), axis=-1, keepdims=True)
            sel.append(am)
            val.append(mx)
            work = jnp.where(lane == am, -jnp.inf, work)
        ex = [jnp.exp(v - val[0]) for v in val]
        inv = 1.0 / (ex[0] + ex[1] + ex[2] + ex[3])

        chosen = jnp.zeros((tm, LANES), F32)
        for am in sel:
            chosen = chosen + jnp.where(lane == am, 1.0, 0.0)
        before = jnp.dot(tri_ref[...], chosen.astype(BF16), preferred_element_type=F32) + cnt_scr[...]
        ridx = jnp.zeros((tm, LANES), F32)
        rgate = jnp.zeros((tm, LANES), F32)
        for k in range(TOP_K):
            rank = jnp.sum(jnp.where(lane == sel[k], before, 0.0), axis=-1, keepdims=True)
            ridx = (ridx + jnp.where(lane == k, sel[k].astype(F32), 0.0)
                    + jnp.where(lane == TOP_K + k, rank, 0.0))
            rgate = rgate + jnp.where(lane == k, ex[k] * inv, 0.0)
        ridx_ref[...] = ridx.astype(jnp.int32)
        rgate_ref[...] = rgate
        cnt_scr[...] = cnt_scr[...] + jnp.sum(chosen, axis=0, keepdims=True)
        cnt_out_ref[...] = cnt_scr[...]

    @pl.when(i < n_p_tiles)
    def _():
        body(xp_ref, yap_ref, ybp_ref)

    @pl.when(i >= n_p_tiles)
    def _():
        body(xs_ref, yas_ref, ybs_ref)


def _out_proj(prompt, sample, wo16, g2, rwh, rwl, rb, tri):
    tm = tri.shape[0]
    n_p, n_s = prompt[0].shape[0], sample[0].shape[0]
    assert n_p % tm == 0 and n_s % tm == 0
    n_p_tiles = n_p // tm
    n = n_p + n_s
    p_spec = lambda w: pl.BlockSpec((tm, w), lambda i: (jnp.minimum(i, n_p_tiles - 1), 0))
    s_spec = lambda w: pl.BlockSpec((tm, w), lambda i: (jnp.maximum(i - n_p_tiles, 0), 0))
    row = lambda w: pl.BlockSpec((tm, w), lambda i: (i, 0))
    widths = (D_MODEL, GM_WIDTH, SSM_WIDTH)
    return pl.pallas_call(
        functools.partial(_out_proj_kernel, tm=tm, n_p_tiles=n_p_tiles),
        grid=(n // tm,),
        in_specs=[p_spec(w) for w in widths] + [s_spec(w) for w in widths] + [
            _const_spec(wo16.shape), _const_spec((1, D_MODEL)), _const_spec(rwh.shape), _const_spec(rwl.shape),
            _const_spec((1, LANES)), _const_spec(tri.shape)],
        out_specs=[row(D_MODEL), row(D_MODEL), row(LANES), row(LANES), _const_spec((1, LANES))],
        out_shape=[jax.ShapeDtypeStruct((n, D_MODEL), F32), jax.ShapeDtypeStruct((n, D_MODEL), F32),
                   jax.ShapeDtypeStruct((n, LANES), jnp.int32), jax.ShapeDtypeStruct((n, LANES), F32),
                   jax.ShapeDtypeStruct((1, LANES), F32)],
        scratch_shapes=[pltpu.VMEM((1, LANES), F32)],
        compiler_params=_params("arbitrary"),
        name="out_proj",
    )(*prompt, *sample, wo16, g2, rwh, rwl, rb, tri)


def _moe_kernel(bexp_ref, nval_ref, nused_ref, tok_ref, slot_ref, h2_hbm, w1g_ref, w1u_ref, b1g_ref, b1u_ref,
                w2_ref, b2_ref, y4_hbm, xbuf, xb16, acc, obuf, gsem, ssem, *, tm, nj):
    del bexp_ref
    b = pl.program_id(0)
    j = pl.program_id(1)
    nv = nval_ref[b]

    def row_in(r):
        return pltpu.make_async_copy(h2_hbm.at[pl.ds(tok_ref[0, 0, r], 1), :], xbuf.at[pl.ds(r, 1), :], gsem)

    def row_out(r):
        return pltpu.make_async_copy(obuf.at[pl.ds(r, 1), :], y4_hbm.at[pl.ds(slot_ref[0, 0, r], 1), :], ssem)

    @pl.when((b == 0) & (j == 0))
    def _():
        xbuf[...] = jnp.zeros_like(xbuf)

    @pl.when(b < nused_ref[0])
    def _():
        @pl.when(j == 0)
        def _():
            lax.fori_loop(0, nv, lambda r, c: (row_in(r).start(), c)[1], 0)
            lax.fori_loop(0, nv, lambda r, c: (row_in(r).wait(), c)[1], 0)
            rid = lax.broadcasted_iota(jnp.int32, (tm, 1), 0)
            xb16[...] = jnp.where(rid < nv, xbuf[...], 0.0).astype(BF16)
            acc[...] = jnp.zeros_like(acc)

        xb = xb16[...]
        hg = jnp.dot(xb, w1g_ref[0].astype(BF16), preferred_element_type=F32) + b1g_ref[0]
        hu = jnp.dot(xb, w1u_ref[0].astype(BF16), preferred_element_type=F32) + b1u_ref[0]
        gate = jnp.minimum(hg, SWIGLU_LIMIT)
        up = jnp.clip(hu, -SWIGLU_LIMIT, SWIGLU_LIMIT)
        act = (up + 1.0) * gate * _sigmoid(SWIGLU_ALPHA * gate)
        acc[...] += jnp.dot(act.astype(BF16), w2_ref[0].astype(BF16), preferred_element_type=F32)

        @pl.when(j == nj - 1)
        def _():
            @pl.when(b > 0)
            def _():
                lax.fori_loop(0, nval_ref[jnp.maximum(b - 1, 0)], lambda r, c: (row_out(0).wait(), c)[1], 0)

            obuf[...] = acc[...] + b2_ref[0]
            lax.fori_loop(0, nv, lambda r, c: (row_out(r).start(), c)[1], 0)

            @pl.when(b == nused_ref[0] - 1)
            def _():
                lax.fori_loop(0, nv, lambda r, c: (row_out(0).wait(), c)[1], 0)


def _moe(h2, w1, b1, w2, b2, bexp, nval, nused, toks, slots, *, n_rows_out):
    nb, _, tm = slots.shape
    tf = MOE_FF_TILE
    nj = D_FF // tf
    live = lambda b, nu: b < nu[0]
    jj = lambda b, j, nu: jnp.where(live(b, nu), j, nj - 1)
    grid_spec = pltpu.PrefetchScalarGridSpec(
        num_scalar_prefetch=3,
        grid=(nb, nj),
        in_specs=[
            pl.BlockSpec((1, 1, tm), lambda b, j, be, nv, nu: (b, 0, 0), memory_space=pltpu.SMEM),
            pl.BlockSpec((1, 1, tm), lambda b, j, be, nv, nu: (b, 0, 0), memory_space=pltpu.SMEM),
            pl.BlockSpec(memory_space=pl.ANY),
            pl.BlockSpec((1, D_MODEL, tf), lambda b, j, be, nv, nu: (be[b], 0, jj(b, j, nu))),
            pl.BlockSpec((1, D_MODEL, tf), lambda b, j, be, nv, nu: (be[b], 0, nj + jj(b, j, nu))),
            pl.BlockSpec((1, 1, tf), lambda b, j, be, nv, nu: (be[b], 0, jj(b, j, nu))),
            pl.BlockSpec((1, 1, tf), lambda b, j, be, nv, nu: (be[b], 0, nj + jj(b, j, nu))),
            pl.BlockSpec((1, tf, D_MODEL), lambda b, j, be, nv, nu: (be[b], jj(b, j, nu), 0)),
            pl.BlockSpec((1, 1, D_MODEL), lambda b, j, be, nv, nu: (be[b], 0, 0)),
        ],
        out_specs=pl.BlockSpec(memory_space=pl.ANY),
        scratch_shapes=[pltpu.VMEM((tm, D_MODEL), F32), pltpu.VMEM((tm, D_MODEL), BF16),
                        pltpu.VMEM((tm, D_MODEL), F32), pltpu.VMEM((tm, D_MODEL), F32),
                        pltpu.SemaphoreType.DMA, pltpu.SemaphoreType.DMA],
    )
    return pl.pallas_call(
        functools.partial(_moe_kernel, tm=tm, nj=nj),
        grid_spec=grid_spec,
        out_shape=jax.ShapeDtypeStruct((n_rows_out, D_MODEL), F32),
        compiler_params=_params("arbitrary", "arbitrary"),
        name="moe",
    )(bexp, nval, nused, toks, slots, h2, w1, w1, b1, b1, w2, b2)


def _combine_kernel(x1_ref, y4_ref, gate_ref, gf_ref, out_ref):
    x = x1_ref[...]
    g = gate_ref[...]
    for k in range(TOP_K):
        x = x + g[:, k:k + 1] * y4_ref[k]
    out_ref[...] = x * _rms_scale(x) * gf_ref[...]


def _combine(x1, y4w, gates, gf, *, row_off, n_rows):
    tc = min(COMBINE_TILE, n_rows)
    assert n_rows % tc == 0 and row_off % tc == 0
    off = row_off // tc
    return pl.pallas_call(
        _combine_kernel,
        grid=(n_rows // tc,),
        in_specs=[pl.BlockSpec((tc, D_MODEL), lambda i: (i + off, 0)),
                  pl.BlockSpec((TOP_K, tc, D_MODEL), lambda i: (0, i + off, 0)),
                  pl.BlockSpec((tc, LANES), lambda i: (i + off, 0)),
                  _const_spec((1, D_MODEL))],
        out_specs=pl.BlockSpec((tc, D_MODEL), lambda i: (i, 0)),
        out_shape=jax.ShapeDtypeStruct((n_rows, D_MODEL), F32),
        compiler_params=_params("arbitrary"),
        name="combine",
    )(x1, y4w, gates, gf)


def _mix_tables(ws, bs, t):
    pos = jnp.arange(GM_CHUNK)
    mask = (pos[None, :] // CHUNK) <= (pos[:, None] // CHUNK)
    wm = jnp.where(mask[None], ws, 0.0)
    if t % GM_CHUNK == 0:
        bias_rows = bs
    else:
        assert GM_CHUNK % t == 0
        rep = GM_CHUNK // t
        eye = jnp.eye(rep, dtype=ws.dtype)
        wm = jnp.einsum("ab,hij->haibj", eye, wm[:, :t, :t]).reshape(GM_HEADS, GM_CHUNK, GM_CHUNK)
        bias_rows = jnp.tile(bs[:, :t], (1, rep))
    bsb = jnp.repeat(bias_rows.T, GM_HEAD_DIM, axis=1)
    return wm.astype(BF16), bsb.astype(F32)


def _s5_tables(lam_re, lam_im, log_dt, b_re, b_im, c_re, c_im, d_skip, glu_w, glu_b, gain_b):
    lr = lam_re.astype(F32)
    li = lam_im.astype(F32)
    dt = jnp.exp(log_dt.astype(F32))[:, None]
    mag = jnp.exp(lr * dt)
    ab_re = mag * jnp.cos(li * dt)
    ab_im = mag * jnp.sin(li * dt)
    den = lr * lr + li * li
    f_re = ((ab_re - 1.0) * lr + ab_im * li) / den
    f_im = (ab_im * lr - (ab_re - 1.0) * li) / den
    br = b_re.astype(F32)
    bi = b_im.astype(F32)
    bb_re = f_re[..., None] * br - f_im[..., None] * bi
    bb_im = f_re[..., None] * bi + f_im[..., None] * br
    per = LANES // SSM_GROUP
    nt = SSM_GROUPS // per
    eye = jnp.eye(per, dtype=F32)

    def in_blocks(bb):
        return jnp.einsum("jgpc,gh->jgchp", bb.reshape(nt, per, SSM_STATE, SSM_GROUP), eye).reshape(
            nt, LANES, per * SSM_STATE).astype(BF16)

    def out_blocks(cc):
        return jnp.einsum("qgcp,gh->qgphc", cc.astype(F32).reshape(nt, per, SSM_GROUP, SSM_STATE), eye).reshape(
            nt, per * SSM_STATE, LANES).astype(BF16)

    bcast = lambda a: jnp.broadcast_to(a.reshape(1, SSM_FLAT), (SUBLANES, SSM_FLAT))
    return dict(ar=bcast(ab_re), ai=bcast(ab_im), wbr=in_blocks(bb_re), wbi=in_blocks(bb_im),
                cr=out_blocks(c_re), ci=out_blocks(c_im), dsk=d_skip.astype(F32).reshape(1, SSM_WIDTH),
                gluw=glu_w.astype(BF16), glub=glu_b.astype(F32).reshape(1, SSM_WIDTH),
                gb=gain_b.astype(F32).reshape(1, SSM_WIDTH))


def _routing_tables(ridx, counts, tm, nb):
    n = ridx.shape[0]
    eidx = ridx[:, :TOP_K]
    rank = ridx[:, TOP_K:2 * TOP_K]
    padded = (counts + tm - 1) // tm * tm
    pad_end = jnp.cumsum(padded)
    pad_start = pad_end - padded
    pos = pad_start[eidx] + rank
    ids = jnp.arange(n * TOP_K, dtype=jnp.int32)
    ids = jnp.zeros((nb * tm,), jnp.int32).at[pos.reshape(-1)].set(ids, unique_indices=True)
    toks = lax.shift_right_logical(ids, 2)
    slots = (ids & (TOP_K - 1)) * n + toks
    nused = (pad_end[-1] // tm).astype(jnp.int32)
    blk = jnp.arange(nb, dtype=jnp.int32)
    first_row = jnp.minimum(blk, nused - 1) * tm
    bexp = jnp.minimum(jnp.sum(pad_end[None, :] <= first_row[:, None], axis=1), N_EXPERTS - 1).astype(jnp.int32)
    nval = jnp.where(blk < nused, jnp.clip(counts[bexp] - (blk * tm - pad_start[bexp]), 0, tm), 0)
    return bexp, nval.astype(jnp.int32), nused.reshape(1), toks.reshape(nb, 1, tm), slots.reshape(nb, 1, tm)


@jax.jit
def kernel(x_prompt, x_sample, state_ssm_re, state_ssm_im, norm1_g, w_in, gm_ln_g, gm_ln_b, gm_ws, gm_bs, ssm_lambda_re, ssm_lambda_im, ssm_log_dt, ssm_b_re, ssm_b_im, ssm_c_re, ssm_c_im, ssm_d, ssm_glu_w, ssm_glu_b, mix_norm_a_g, mix_norm_b_g, w_out, norm2_g, router_w, router_b, moe_w1, moe_b1, moe_w2, moe_b2, final_norm_g):
    assert w_in.shape[0] == 1, "single-layer model"
    bp, sp_len, _ = x_prompt.shape
    bs_, ss_len, _ = x_sample.shape
    n_p, n_s = bp * sp_len, bs_ * ss_len
    n = n_p + n_s
    row = lambda a, w: a.astype(F32).reshape(1, w)

    win16 = w_in[0].astype(BF16)
    wo16 = w_out[0].astype(BF16)
    s5p = _s5_tables(ssm_lambda_re[0], ssm_lambda_im[0], ssm_log_dt[0], ssm_b_re[0], ssm_b_im[0], ssm_c_re[0],
                     ssm_c_im[0], ssm_d[0], ssm_glu_w[0], ssm_glu_b[0], mix_norm_b_g[0])
    rw = jnp.pad(router_w[0].astype(F32), ((0, 0), (0, LANES - N_EXPERTS)))
    rwh = rw.astype(BF16)
    rwl = (rw - rwh.astype(F32)).astype(BF16)
    rb = jnp.pad(router_b[0].astype(F32), (0, LANES - N_EXPERTS), constant_values=-1e30).reshape(1, LANES)
    tm = min(TOKEN_TILE, n_s, n_p)
    tri = (jnp.arange(tm)[:, None] > jnp.arange(tm)[None, :]).astype(BF16)

    streams = []
    for x, t, h0 in ((x_prompt, sp_len, None), (x_sample, ss_len, (state_ssm_re[0], state_ssm_im[0]))):
        nb = x.shape[0]
        x2d = x.reshape(nb * t, D_MODEL)
        wmix16, bsb = _mix_tables(gm_ws[0], gm_bs[0], t)
        emit_v = h0 is not None
        outs = _in_proj(x2d, row(norm1_g[0], D_MODEL), win16, row(gm_ln_g[0], GM_WIDTH), row(gm_ln_b[0], GM_WIDTH),
                        wmix16, bsb, row(mix_norm_a_g[0], GM_WIDTH), emit_v=emit_v)
        ya, zs = outs[0], outs[1]
        v_rows = outs[2] if emit_v else None
        if h0 is None:
            h0r = h0i = jnp.zeros((nb, SSM_FLAT), F32)
        else:
            h0r = h0[0].astype(F32).reshape(nb, SSM_FLAT)
            h0i = h0[1].astype(F32).reshape(nb, SSM_FLAT)
        yb3, hr, hi = _s5(zs.reshape(nb, t, SSM_WIDTH), h0r, h0i, s5p)
        streams.append(dict(x2d=x2d, ya=ya, yb=yb3.reshape(nb * t, SSM_WIDTH), hr=hr, hi=hi, v=v_rows, nb=nb, t=t))

    sp_, ss_ = streams
    x1, h2, ridx, rgate, cnt = _out_proj((sp_["x2d"], sp_["ya"], sp_["yb"]), (ss_["x2d"], ss_["ya"], ss_["yb"]),
                                         wo16, row(norm2_g[0], D_MODEL), rwh, rwl, rb, tri)
    counts = cnt[0, :N_EXPERTS].astype(jnp.int32)
    nblk = -(-(n * TOP_K) // MOE_ROWS) + N_EXPERTS
    bexp, nval, nused, toks, slots = _routing_tables(ridx, counts, MOE_ROWS, nblk)
    y4 = _moe(h2, moe_w1[0], moe_b1[0].reshape(N_EXPERTS, 1, 2 * D_FF), moe_w2[0],
              moe_b2[0].reshape(N_EXPERTS, 1, D_MODEL), bexp, nval, nused, toks, slots, n_rows_out=n * TOP_K)
    y4w = y4.reshape(TOP_K, n, D_MODEL)

    outs = []
    for st, off in ((sp_, 0), (ss_, n_p)):
        y = _combine(x1, y4w, rgate, row(final_norm_g, D_MODEL), row_off=off, n_rows=st["nb"] * st["t"])
        outs.append(y.reshape(st["nb"], st["t"], D_MODEL))
    state = lambda a, nb: a.reshape(1, nb, SSM_GROUPS, SSM_STATE)
    return (outs[0], outs[1], state(sp_["hr"], bp), state(sp_["hi"], bp), state(ss_["hr"], bs_),
            state(ss_["hi"], bs_), ss_["v"].reshape(1, bs_, ss_len, GM_WIDTH))
```
